```python
import jax, jax.numpy as jnp
from jax import lax
import numpy as np

D_MODEL = 1024
BATCH = 8
SEQ = 4096
DEPTH = 2
DEC_BATCH = 4
DEC_SEQ = 4096
PAST_LEN = 128

CHUNK = 128
A_HEADS = 4
A_HEAD_DIM = 128
A_WIDTH = A_HEADS * A_HEAD_DIM
B_WINDOWS = (2, 4, 8, 16)
B_GROUPS = len(B_WINDOWS)
B_GROUP_DIM = 128
B_WIDTH = B_GROUPS * B_GROUP_DIM
AB_IN = 2 * A_WIDTH + B_WIDTH
AB_OUT = A_WIDTH + B_WIDTH
C_WIDTH = D_MODEL
CONV_W = 3
D_FF = -(-8 * D_MODEL // (3 * 256)) * 256
N_AB_LAYERS = (DEPTH + 1) // 2
N_C_LAYERS = DEPTH // 2
EPS = 1e-6

kernel_name = "hybrid_gmlp_pool_shortconv_encoder"


def rmsnorm(x, g):
    xf = x.astype(jnp.float32)
    y = xf * lax.rsqrt(jnp.mean(xf * xf, axis=-1, keepdims=True) + EPS)
    return (y * g.astype(jnp.float32)).astype(x.dtype)


def chunk_spatial_gate(u, v, g_v, w_s, b_s):
    bn, s, _ = v.shape
    n_c = s // CHUNK
    vf = v.reshape(bn, n_c, CHUNK, A_HEADS, A_HEAD_DIM).astype(jnp.float32)
    mu = jnp.mean(vf, axis=-1, keepdims=True)
    var = jnp.mean(jnp.square(vf - mu), axis=-1, keepdims=True)
    vn = ((vf - mu) * lax.rsqrt(var + EPS) * g_v.reshape(A_HEADS, A_HEAD_DIM).astype(jnp.float32)).astype(v.dtype)
    mixed = jnp.einsum('hpq,bcqhd->bcphd', w_s, vn) + jnp.transpose(b_s)[:, :, None]
    return u * mixed.reshape(bn, s, A_WIDTH)


def multiscale_pool(z, w_pool, pool_scale):
    s = z.shape[1]
    pos = jnp.arange(s)
    outs = []
    for gi, w in enumerate(B_WINDOWS):
        h = w // 2
        zg = z[..., gi * B_GROUP_DIM:(gi + 1) * B_GROUP_DIM].astype(jnp.float32)
        cs = jnp.cumsum(jnp.pad(zg, ((0, 0), (h + 1, h), (0, 0))), axis=1)
        win_sum = cs[:, w:w + s] - cs[:, 0:s]
        count = (jnp.minimum(pos + h, s) - jnp.maximum(pos - h, 0)).astype(jnp.float32)
        r = (win_sum / count[None, :, None] - zg).astype(z.dtype)
        outs.append(jnp.einsum('bsc,cd->bsd', r, w_pool[gi]))
    return jnp.concatenate(outs, axis=-1) * pool_scale


def mixer_ab(x, w_in, g_v, w_s, b_s, w_pool, pool_scale, w_out):
    hcomb = jnp.einsum('bsd,de->bse', x, w_in)
    u = jax.nn.gelu(hcomb[..., :A_WIDTH])
    v = jax.nn.gelu(hcomb[..., A_WIDTH:2 * A_WIDTH])
    z = hcomb[..., 2 * A_WIDTH:]
    a = chunk_spatial_gate(u, v, g_v, w_s, b_s)
    b = multiscale_pool(z, w_pool, pool_scale)
    return jnp.einsum('bse,ed->bsd', jnp.concatenate([a, b], axis=-1), w_out)


def mixer_c(x, w_in, conv_w, w_out):
    s = x.shape[1]
    hcomb = jnp.einsum('bsd,de->bse', x, w_in)
    gate_b = hcomb[..., :C_WIDTH]
    gate_c = hcomb[..., C_WIDTH:2 * C_WIDTH]
    hv = hcomb[..., 2 * C_WIDTH:]
    t = jnp.pad(gate_c * hv, ((0, 0), (1, 1), (0, 0)))
    conv = t[:, 0:s] * conv_w[0] + t[:, 1:s + 1] * conv_w[1] + t[:, 2:s + 2] * conv_w[2]
    return jnp.einsum('bse,ed->bsd', gate_b * conv, w_out)


def swiglu(x, w_gate, w_up, w_down):
    hg = jnp.einsum('bsd,df->bsf', x, w_gate)
    hu = jnp.einsum('bsd,df->bsf', x, w_up)
    return jnp.einsum('bsf,fd->bsd', jax.nn.silu(hg) * hu, w_down)


def trunk(x, norm_g, ab_w_in, ab_v_norm_g, ab_w_spatial, ab_b_spatial, ab_w_pool,
          ab_pool_scale, ab_w_out, c_w_in, c_conv_w, c_w_out, ffn_w_gate, ffn_w_up, ffn_w_down):
    for layer in range(DEPTH):
        g = norm_g[layer]
        h = rmsnorm(x, g[0])
        i = layer // 2
        if layer % 2 == 0:
            h = mixer_ab(h, ab_w_in[i], ab_v_norm_g[i], ab_w_spatial[i], ab_b_spatial[i],
                         ab_w_pool[i], ab_pool_scale[i], ab_w_out[i])
        else:
            h = mixer_c(h, c_w_in[i], c_conv_w[i], c_w_out[i])
        x = x + rmsnorm(h, g[1])
        h = swiglu(rmsnorm(x, g[2]), ffn_w_gate[layer], ffn_w_up[layer], ffn_w_down[layer])
        x = x + rmsnorm(h, g[3])
    return x


def setup_inputs(seed: int = 0) -> dict:
    key = jax.random.key(seed)
    ks = jax.random.split(key, 17)
    f32 = jnp.float32
    nrm = lambda k, shape, scale: jax.random.normal(k, shape, f32) * scale
    return {
        "x_prompt": jax.random.normal(ks[0], (BATCH, SEQ, D_MODEL), f32),
        "x_sample": jax.random.normal(ks[1], (DEC_BATCH, DEC_SEQ, D_MODEL), f32),
        "norm_g": 1.0 + nrm(ks[2], (DEPTH, 4, D_MODEL), 0.02),
        "ab_w_in": nrm(ks[3], (N_AB_LAYERS, D_MODEL, AB_IN), D_MODEL ** -0.5),
        "ab_v_norm_g": 1.0 + nrm(ks[4], (N_AB_LAYERS, A_WIDTH), 0.02),
        "ab_w_spatial": nrm(ks[5], (N_AB_LAYERS, A_HEADS, CHUNK, CHUNK), CHUNK ** -0.5),
        "ab_b_spatial": 1.0 + nrm(ks[6], (N_AB_LAYERS, A_HEADS, CHUNK), 0.02),
        "ab_w_pool": nrm(ks[7], (N_AB_LAYERS, B_GROUPS, B_GROUP_DIM, B_GROUP_DIM), B_GROUP_DIM ** -0.5),
        "ab_pool_scale": 1.0 + nrm(ks[8], (N_AB_LAYERS, B_WIDTH), 0.02),
        "ab_w_out": nrm(ks[9], (N_AB_LAYERS, AB_OUT, D_MODEL), AB_OUT ** -0.5),
        "c_w_in": nrm(ks[10], (N_C_LAYERS, D_MODEL, 3 * C_WIDTH), D_MODEL ** -0.5),
        "c_conv_w": nrm(ks[11], (N_C_LAYERS, CONV_W, C_WIDTH), CONV_W ** -0.5),
        "c_w_out": nrm(ks[12], (N_C_LAYERS, C_WIDTH, D_MODEL), C_WIDTH ** -0.5),
        "ffn_w_gate": nrm(ks[13], (DEPTH, D_MODEL, D_FF), D_MODEL ** -0.5),
        "ffn_w_up": nrm(ks[14], (DEPTH, D_MODEL, D_FF), D_MODEL ** -0.5),
        "ffn_w_down": nrm(ks[15], (DEPTH, D_FF, D_MODEL), D_FF ** -0.5),
    }


def reference(x_prompt, x_sample, norm_g, ab_w_in, ab_v_norm_g, ab_w_spatial, ab_b_spatial,
              ab_w_pool, ab_pool_scale, ab_w_out, c_w_in, c_conv_w, c_w_out,
              ffn_w_gate, ffn_w_up, ffn_w_down):
    y_prompt = trunk(x_prompt, norm_g, ab_w_in, ab_v_norm_g, ab_w_spatial, ab_b_spatial, ab_w_pool,
                     ab_pool_scale, ab_w_out, c_w_in, c_conv_w, c_w_out, ffn_w_gate, ffn_w_up, ffn_w_down)
    y_sample = trunk(x_sample, norm_g, ab_w_in, ab_v_norm_g, ab_w_spatial, ab_b_spatial, ab_w_pool,
                     ab_pool_scale, ab_w_out, c_w_in, c_conv_w, c_w_out, ffn_w_gate, ffn_w_up, ffn_w_down)
    return (y_prompt, y_sample)
```

```python
import functools

import jax
import jax.numpy as jnp
from jax import lax
from jax.experimental import pallas as pl
from jax.experimental.pallas import tpu as pltpu

D_MODEL = 1024
CHUNK = 128
A_HEADS = 4
A_HEAD_DIM = 128
A_WIDTH = A_HEADS * A_HEAD_DIM
B_WINDOWS = (2, 4, 8, 16)
B_GROUP_DIM = 128
B_WIDTH = len(B_WINDOWS) * B_GROUP_DIM
C_WIDTH = D_MODEL
D_FF = 2816
EPS = 1e-6

TILE = 512
HALO = 16
FFN_CHUNK = 256
VMEM_LIMIT_BYTES = 60 * 1024 * 1024

F32 = jnp.float32
BF16 = jnp.bfloat16


def _dot(a, b):
    return jnp.dot(a, b, preferred_element_type=F32)


def _rmsnorm(x, g):
    return x * lax.rsqrt(jnp.mean(x * x, axis=-1, keepdims=True) + EPS) * g


def _swiglu(hn, wg_ref, wu_ref, wd_ref):
    acc = None
    for j in range(0, D_FF, FFN_CHUNK):
        hg = _dot(hn, wg_ref[:, j:j + FFN_CHUNK])
        hu = _dot(hn, wu_ref[:, j:j + FFN_CHUNK])
        act = (jax.nn.silu(hg) * hu).astype(BF16)
        part = _dot(act, wd_ref[j:j + FFN_CHUNK, :])
        acc = part if acc is None else acc + part
    return acc


def _extended_tile(xp_ref, x_ref, xn_ref):
    i = pl.program_id(1)
    last = pl.num_programs(1) - 1
    xp = jnp.where(i > 0, xp_ref[0], 0.0)
    xn = jnp.where(i < last, xn_ref[0], 0.0)
    return jnp.concatenate([xp, x_ref[0], xn], axis=0)


def _finish_layer(x, mixed, g, wg_ref, wu_ref, wd_ref, o_ref):
    x1 = x + _rmsnorm(mixed, g[1:2])
    hn = _rmsnorm(x1, g[2:3]).astype(BF16)
    hf = _swiglu(hn, wg_ref, wu_ref, wd_ref)
    o_ref[0] = x1 + _rmsnorm(hf, g[3:4])


def _layer_ab_kernel(xp_ref, x_ref, xn_ref, g_ref, win_ref, gv_ref, ws_ref, bs_ref,
                     wpool_ref, pscale_ref, wout_ref, wg_ref, wu_ref, wd_ref, o_ref,
                     *, seq_len):
    tile = x_ref.shape[1]
    n_ext = tile + 2 * HALO
    g = g_ref[...]
    x = x_ref[0]
    h_ext = _rmsnorm(_extended_tile(xp_ref, x_ref, xn_ref), g[0:1]).astype(BF16)
    hc = _dot(h_ext, win_ref[...])
    u = jax.nn.gelu(hc[HALO:HALO + tile, :A_WIDTH])
    v = jax.nn.gelu(hc[HALO:HALO + tile, A_WIDTH:2 * A_WIDTH])
    z_ext = hc[:, 2 * A_WIDTH:]

    gv = gv_ref[...]
    a_parts = []
    for hd in range(A_HEADS):
        cols = slice(hd * A_HEAD_DIM, (hd + 1) * A_HEAD_DIM)
        vh = v[:, cols]
        mu = jnp.mean(vh, axis=-1, keepdims=True)
        var = jnp.mean(jnp.square(vh - mu), axis=-1, keepdims=True)
        vn = ((vh - mu) * lax.rsqrt(var + EPS) * gv[:, cols]).astype(BF16)
        w_s = ws_ref[hd]
        b_s = bs_ref[hd]
        rows = [_dot(w_s, vn[c:c + CHUNK]) + b_s for c in range(0, tile, CHUNK)]
        a_parts.append(u[:, cols] * jnp.concatenate(rows, axis=0))

    pos = pl.program_id(1) * tile + lax.broadcasted_iota(jnp.int32, (tile, 1), 0)
    b_parts = []
    for gi, w in enumerate(B_WINDOWS):
        half = w // 2
        cols = slice(gi * B_GROUP_DIM, (gi + 1) * B_GROUP_DIM)
        zg = z_ext[:, cols]
        s = zg + pltpu.roll(zg, 1, 0)
        k = 1
        while k < half:
            s = pltpu.roll(s, k, 0) + pltpu.roll(s, n_ext - k, 0)
            k *= 2
        count = (jnp.minimum(pos + half, seq_len) - jnp.maximum(pos - half, 0)).astype(F32)
        r = s[HALO:HALO + tile] / count - zg[HALO:HALO + tile]
        b_parts.append(_dot(r.astype(BF16), wpool_ref[gi]) * pscale_ref[:, cols])

    ab = jnp.concatenate(a_parts + b_parts, axis=1).astype(BF16)
    _finish_layer(x, _dot(ab, wout_ref[...]), g, wg_ref, wu_ref, wd_ref, o_ref)


def _layer_c_kernel(xp_ref, x_ref, xn_ref, g_ref, win_ref, convw_ref, wout_ref,
                    wg_ref, wu_ref, wd_ref, o_ref):
    tile = x_ref.shape[1]
    n_ext = tile + 2 * HALO
    g = g_ref[...]
    x = x_ref[0]
    h_ext = _rmsnorm(_extended_tile(xp_ref, x_ref, xn_ref), g[0:1]).astype(BF16)
    gate_b = _dot(h_ext[HALO:HALO + tile], win_ref[:, :C_WIDTH])
    cv = _dot(h_ext, win_ref[:, C_WIDTH:])
    t = cv[:, :C_WIDTH] * cv[:, C_WIDTH:]
    cw = convw_ref[...]
    conv = (pltpu.roll(t, 1, 0)[HALO:HALO + tile] * cw[0:1]
            + t[HALO:HALO + tile] * cw[1:2]
            + pltpu.roll(t, n_ext - 1, 0)[HALO:HALO + tile] * cw[2:3])
    y = (gate_b * conv).astype(BF16)
    _finish_layer(x, _dot(y, wout_ref[...]), g, wg_ref, wu_ref, wd_ref, o_ref)


def _resident(shape):
    zeros = (0,) * len(shape)
    return pl.BlockSpec(shape, lambda b, i: zeros, pipeline_mode=pl.Buffered(1))


def _run_layer(body, x, params, name):
    bsz, seq_len, d = x.shape
    tile = TILE
    n_tiles = seq_len // tile
    halo_per_tile = tile // HALO
    last_halo = seq_len // HALO - 1
    x_specs = [
        pl.BlockSpec((1, HALO, d), lambda b, i: (b, jnp.maximum(i * halo_per_tile - 1, 0), 0)),
        pl.BlockSpec((1, tile, d), lambda b, i: (b, i, 0)),
        pl.BlockSpec((1, HALO, d), lambda b, i: (b, jnp.minimum((i + 1) * halo_per_tile, last_halo), 0)),
    ]
    return pl.pallas_call(
        body,
        grid=(bsz, n_tiles),
        in_specs=x_specs + [_resident(p.shape) for p in params],
        out_specs=pl.BlockSpec((1, tile, d), lambda b, i: (b, i, 0)),
        out_shape=jax.ShapeDtypeStruct(x.shape, x.dtype),
        compiler_params=pltpu.CompilerParams(
            dimension_semantics=("parallel", "parallel"),
            vmem_limit_bytes=VMEM_LIMIT_BYTES),
        name=name,
    )(x, x, x, *params)


def kernel(x_prompt, x_sample, norm_g, ab_w_in, ab_v_norm_g, ab_w_spatial, ab_b_spatial, ab_w_pool, ab_pool_scale, ab_w_out, c_w_in, c_conv_w, c_w_out, ffn_w_gate, ffn_w_up, ffn_w_down):
    seq_len = x_prompt.shape[1]
    assert x_sample.shape[1] == seq_len and seq_len % TILE == 0 and TILE % CHUNK == 0
    bf = lambda w: w.astype(BF16)
    ffn = lambda l: (bf(ffn_w_gate[l]), bf(ffn_w_up[l]), bf(ffn_w_down[l]))
    b_s = jnp.broadcast_to(ab_b_spatial[0][:, :, None], (A_HEADS, CHUNK, A_HEAD_DIM))
    params_ab = (norm_g[0], bf(ab_w_in[0]), ab_v_norm_g[0][None, :], bf(ab_w_spatial[0]), b_s,
                 bf(ab_w_pool[0]), ab_pool_scale[0][None, :], bf(ab_w_out[0])) + ffn(0)
    params_c = (norm_g[1], bf(c_w_in[0]), c_conv_w[0], bf(c_w_out[0])) + ffn(1)
    body_ab = functools.partial(_layer_ab_kernel, seq_len=seq_len)

    def trunk(x):
        x = _run_layer(body_ab, x, params_ab, "layer_ab")
        return _run_layer(_layer_c_kernel, x, params_c, "layer_c")

    return (trunk(x_prompt), trunk(x_sample))
```
